```python
import jax, jax.numpy as jnp
from jax import lax
import numpy as np

D_MODEL = 2048
BATCH = 8
SEQ = 2048
DEPTH = 1

GRID_W = 64
MIX_W = D_MODEL
RWKV_W = MIX_W // 2
ATTN_W = MIX_W - RWKV_W
HEAD_DIM = 64
RWKV_HEADS = RWKV_W // HEAD_DIM
ATTN_Q_HEADS = ATTN_W // HEAD_DIM
ATTN_KV_HEADS = 4
KV_GROUPS = ATTN_Q_HEADS // ATTN_KV_HEADS
KV_W = ATTN_KV_HEADS * HEAD_DIM
DECAY_LORA = 96
AAA_LORA = 96
GATE_LORA = 256
GN_EPS = 64e-5
ROPE_THETA = 10000.0
ROPE_AXIS_DIM = HEAD_DIM // 2
Q_BLOCK = 128
NORM_EPS = 1e-6
PEER_HEADS = 8
PEER_KEY_DIM = 256
PEER_HALF = PEER_KEY_DIM // 2
N_KEYS = 128
N_EXPERTS = N_KEYS * N_KEYS
PEER_TOPK = 16
PEER_CHUNK = 128

RWKV_SPLITS = (RWKV_W, 2 * RWKV_W, 3 * RWKV_W,
               3 * RWKV_W + DECAY_LORA, 3 * RWKV_W + 2 * DECAY_LORA,
               3 * RWKV_W + 2 * DECAY_LORA + AAA_LORA, 3 * RWKV_W + 2 * DECAY_LORA + 2 * AAA_LORA)
RWKV_COLS = 3 * RWKV_W + 2 * DECAY_LORA + 2 * AAA_LORA + GATE_LORA
IN_SPLITS = (RWKV_COLS, RWKV_COLS + ATTN_W, RWKV_COLS + ATTN_W + KV_W)
IN_COLS = RWKV_COLS + ATTN_W + 2 * KV_W

kernel_name = "hybrid_rwkv7_axialgqa_peer_encoder"


def rmsnorm(x, w):
    xf = x.astype(jnp.float32)
    y = xf * lax.rsqrt(jnp.mean(xf * xf, axis=-1, keepdims=True) + NORM_EPS)
    return (y * w.astype(jnp.float32)).astype(x.dtype)


def centred_shift(z, mu_prev, mu_next):
    zp = jnp.pad(z[:, :-1], ((0, 0), (1, 0), (0, 0)))
    zn = jnp.pad(z[:, 1:], ((0, 0), (0, 1), (0, 0)))
    return z + mu_prev * (zp - z) + mu_next * (zn - z)


def rwkv7_bidir(z, mu_prev, mu_next, w0, w2, a0, a2, g2, k_k, k_a, r_k, lnx_w, lnx_b):
    B, S, _ = z.shape
    H, N = RWKV_HEADS, HEAD_DIM
    f32 = jnp.float32
    z = centred_shift(z.astype(f32), mu_prev.astype(f32), mu_next.astype(f32))
    r, k, v, wl_f, wl_b, al_f, al_b, gl = jnp.split(z, RWKV_SPLITS, axis=-1)
    wl = jnp.stack([wl_f, wl_b])
    al = jnp.stack([al_f, al_b])
    w_log = -jax.nn.softplus(-(w0.astype(f32)[:, None, None, :]
                               + jnp.einsum('nbsl,nlc->nbsc', jnp.tanh(wl), w2.astype(f32)))) - 0.5
    decay = jnp.exp(-jnp.exp(w_log))
    a = jax.nn.sigmoid(a0.astype(f32)[:, None, None, :]
                       + jnp.einsum('nbsl,nlc->nbsc', al, a2.astype(f32)))
    g = jax.nn.sigmoid(gl) @ g2.astype(f32)
    kk = (k * k_k.astype(f32)).reshape(B, S, H, N)
    kk = kk / jnp.maximum(jnp.sqrt(jnp.sum(kk * kk, axis=-1, keepdims=True)), 1e-12)
    kk = kk.reshape(B, S, RWKV_W)
    k_dir = k[None] * (1.0 + (a - 1.0) * k_a.astype(f32))
    ka = kk[None] * a

    def dir_seq(arr):
        arr = jnp.stack([arr[0], arr[1][:, ::-1]])
        return jnp.moveaxis(arr.reshape(2, B, S, H, N), 2, 0)

    xs = (dir_seq(jnp.stack([r, r])), dir_seq(decay), dir_seq(k_dir),
          dir_seq(jnp.stack([v, v])), dir_seq(jnp.stack([kk, kk])), dir_seq(ka))

    def step(state, inp):
        r_t, w_t, k_t, v_t, kk_t, ka_t = inp
        sa = jnp.einsum('dbhij,dbhj->dbhi', state, -kk_t)
        state = (state * w_t[..., None, :] + sa[..., :, None] * ka_t[..., None, :]
                 + v_t[..., :, None] * k_t[..., None, :])
        y = jnp.einsum('dbhij,dbhj->dbhi', state, r_t)
        return state, y

    state0 = jnp.zeros((2, B, H, N, N), f32)
    _, ys = lax.scan(step, state0, xs)
    ys = jnp.moveaxis(ys, 0, 2)
    y = ys[0] + ys[1][:, ::-1]
    mu = jnp.mean(y, axis=-1, keepdims=True)
    var = jnp.mean(jnp.square(y - mu), axis=-1, keepdims=True)
    y = ((y - mu) * lax.rsqrt(var + GN_EPS)).reshape(B, S, RWKV_W)
    y = y * lnx_w.astype(f32) + lnx_b.astype(f32)
    k_mean = jnp.mean(k_dir, axis=0).reshape(B, S, H, N)
    bonus = jnp.sum(r.reshape(B, S, H, N) * k_mean * r_k.astype(f32), axis=-1, keepdims=True) \
        * v.reshape(B, S, H, N)
    return (y + bonus.reshape(B, S, RWKV_W)) * g


def axial_rope_tables(S):
    rows_count = S // GRID_W
    rows = jnp.repeat(jnp.arange(rows_count), GRID_W).astype(jnp.float32)
    cols = jnp.tile(jnp.arange(GRID_W), rows_count).astype(jnp.float32)
    inv = ROPE_THETA ** (-jnp.arange(0, ROPE_AXIS_DIM, 2, dtype=jnp.float32) / ROPE_AXIS_DIM)
    ang_r = rows[:, None] * inv[None]
    ang_c = cols[:, None] * inv[None]
    return jnp.cos(ang_r), jnp.sin(ang_r), jnp.cos(ang_c), jnp.sin(ang_c)


def rope_rotate(x, cos, sin):
    x1, x2 = jnp.split(x, 2, axis=-1)
    c, s = cos[None, :, None, :], sin[None, :, None, :]
    return jnp.concatenate([x1 * c - x2 * s, x2 * c + x1 * s], axis=-1)


def axial_rope(x, tabs):
    cr, sr, cc, sc = tabs
    xr, xc = x[..., :ROPE_AXIS_DIM], x[..., ROPE_AXIS_DIM:]
    return jnp.concatenate([rope_rotate(xr, cr, sr), rope_rotate(xc, cc, sc)], axis=-1)


def axial_gqa(q, k, v, q_norm_w, k_norm_w):
    B, S, _ = q.shape
    dt = q.dtype
    q = q.reshape(B, S, ATTN_Q_HEADS, HEAD_DIM)
    k = k.reshape(B, S, ATTN_KV_HEADS, HEAD_DIM)
    v = v.reshape(B, S, ATTN_KV_HEADS, HEAD_DIM)
    tabs = axial_rope_tables(S)
    q = axial_rope(rmsnorm(q, q_norm_w).astype(jnp.float32), tabs) * (HEAD_DIM ** -0.5)
    k = axial_rope(rmsnorm(k, k_norm_w).astype(jnp.float32), tabs)
    q = q.astype(dt).reshape(B, S // Q_BLOCK, Q_BLOCK, ATTN_KV_HEADS, KV_GROUPS, HEAD_DIM)
    q = jnp.moveaxis(q, 1, 0)
    k = k.astype(dt)

    def block(q_blk):
        s = jnp.einsum('bqkgd,bskd->bkgqs', q_blk, k).astype(jnp.float32)
        p = jax.nn.softmax(s, axis=-1)
        return jnp.einsum('bkgqs,bskd->bqkgd', p.astype(v.dtype), v)

    o = lax.map(block, q)
    return jnp.moveaxis(o, 0, 1).reshape(B, S, ATTN_W)


def peer_ffn(h, w_pq, sub_keys, u_tab, v_tab):
    B, S, Dm = h.shape
    T = B * S
    hf = h.reshape(T, Dm)
    q = (hf @ w_pq).reshape(T, PEER_HEADS, 2, PEER_HALF)
    s = jnp.einsum('thpc,hpnc->thpn', q, sub_keys).astype(jnp.float32)
    s1, i1 = lax.top_k(s[:, :, 0], PEER_TOPK)
    s2, i2 = lax.top_k(s[:, :, 1], PEER_TOPK)
    cand = (s1[..., :, None] + s2[..., None, :]).reshape(T, PEER_HEADS, PEER_TOPK * PEER_TOPK)
    best, ci = lax.top_k(cand, PEER_TOPK)
    e1 = jnp.take_along_axis(i1, ci // PEER_TOPK, axis=-1)
    e2 = jnp.take_along_axis(i2, ci % PEER_TOPK, axis=-1)
    expert = (e1 * N_KEYS + e2).reshape(T, PEER_HEADS * PEER_TOPK)
    gate = jax.nn.softmax(best, axis=-1).reshape(T, PEER_HEADS * PEER_TOPK).astype(h.dtype)
    nc = T // PEER_CHUNK

    def chunk(args):
        hc, ec, gc = args
        u = jnp.take(u_tab, ec, axis=0)
        act = jax.nn.gelu(jnp.einsum('cd,ckd->ck', hc, u), approximate=False) * gc
        vv = jnp.take(v_tab, ec, axis=0)
        return jnp.einsum('ck,ckd->cd', act, vv)

    out = lax.map(chunk, (hf.reshape(nc, PEER_CHUNK, Dm),
                          expert.reshape(nc, PEER_CHUNK, -1),
                          gate.reshape(nc, PEER_CHUNK, -1)))
    return out.reshape(B, S, Dm)


def setup_inputs(seed: int = 0) -> dict:
    key = jax.random.key(seed)
    ks = jax.random.split(key, 32)
    L, D, C = DEPTH, D_MODEL, RWKV_W
    nrm = lambda k, shape, scale: jax.random.normal(k, shape, jnp.float32) * scale
    return {
        "x": nrm(ks[0], (BATCH, SEQ, D), 1.0),
        "w_in": nrm(ks[1], (L, D, IN_COLS), D ** -0.5),
        "mu_prev": jax.random.uniform(ks[2], (L, RWKV_COLS), jnp.float32, 0.0, 0.5),
        "mu_next": jax.random.uniform(ks[3], (L, RWKV_COLS), jnp.float32, 0.0, 0.5),
        "w0": jax.random.uniform(ks[4], (L, 2, C), jnp.float32, -6.0, -1.0),
        "w2": nrm(ks[5], (L, 2, DECAY_LORA, C), 0.5 * DECAY_LORA ** -0.5),
        "a0": nrm(ks[6], (L, 2, C), 0.1),
        "a2": nrm(ks[7], (L, 2, AAA_LORA, C), AAA_LORA ** -0.5),
        "g2": nrm(ks[8], (L, GATE_LORA, C), GATE_LORA ** -0.5),
        "k_k": 0.85 + nrm(ks[9], (L, C), 0.05),
        "k_a": 1.0 + nrm(ks[10], (L, C), 0.05),
        "r_k": nrm(ks[11], (L, RWKV_HEADS, HEAD_DIM), 0.1),
        "lnx_w": 1.0 + nrm(ks[12], (L, C), 0.02),
        "lnx_b": nrm(ks[13], (L, C), 0.01),
        "q_norm_w": 1.0 + nrm(ks[14], (L, HEAD_DIM), 0.02),
        "k_norm_w": 1.0 + nrm(ks[15], (L, HEAD_DIM), 0.02),
        "w_out": nrm(ks[16], (L, MIX_W, D), MIX_W ** -0.5),
        "norm1_w": 1.0 + nrm(ks[17], (L, D), 0.02),
        "norm2_w": 1.0 + nrm(ks[18], (L, D), 0.02),
        "w_pq": nrm(ks[19], (L, D, PEER_HEADS * PEER_KEY_DIM), D ** -0.5),
        "sub_keys": nrm(ks[20], (L, PEER_HEADS, 2, N_KEYS, PEER_HALF), PEER_HALF ** -0.5),
        "u_tab": nrm(ks[21], (L, N_EXPERTS, D), D ** -0.5),
        "v_tab": nrm(ks[22], (L, N_EXPERTS, D), 0.1),
        "normf_w": 1.0 + nrm(ks[23], (D,), 0.02),
    }


def reference(x, w_in, mu_prev, mu_next, w0, w2, a0, a2, g2, k_k, k_a, r_k, lnx_w, lnx_b,
              q_norm_w, k_norm_w, w_out, norm1_w, norm2_w, w_pq, sub_keys, u_tab, v_tab, normf_w):
    for l in range(DEPTH):
        h = rmsnorm(x, norm1_w[l])
        p = h @ w_in[l]
        z_rwkv, q, k, v = jnp.split(p, IN_SPLITS, axis=-1)
        y_rwkv = rwkv7_bidir(z_rwkv, mu_prev[l], mu_next[l], w0[l], w2[l], a0[l], a2[l], g2[l],
                             k_k[l], k_a[l], r_k[l], lnx_w[l], lnx_b[l]).astype(x.dtype)
        y_attn = axial_gqa(q, k, v, q_norm_w[l], k_norm_w[l])
        x = x + jnp.concatenate([y_rwkv, y_attn], axis=-1) @ w_out[l]
        h2 = rmsnorm(x, norm2_w[l])
        x = x + peer_ffn(h2, w_pq[l], sub_keys[l], u_tab[l], v_tab[l])
    return rmsnorm(x, normf_w)
```

```python
import functools

import jax
import jax.numpy as jnp
from jax import lax
from jax.experimental import pallas as pl
from jax.experimental.pallas import tpu as pltpu

F32 = jnp.float32
BF16 = jnp.bfloat16

HEAD_DIM = 64
RWKV_W = 1024
RWKV_HEADS = 16
DECAY_LORA = 96
AAA_LORA = 96
GATE_LORA = 256
LORA_W = 2 * DECAY_LORA + 2 * AAA_LORA + GATE_LORA
RWKV_COLS = 3 * RWKV_W + LORA_W
ATTN_W = 1024
KV_W = 256
ATTN_Q_HEADS = 16
ATTN_KV_HEADS = 4
GRID_W = 64
ROPE_THETA = 10000.0
ROPE_AXIS_DIM = 32
GN_EPS = 64e-5
NORM_EPS = 1e-6
PEER_HEADS = 8
PEER_HALF = 128
N_KEYS = 128
PEER_TOPK = 16
PEER_SLOTS = PEER_HEADS * PEER_TOPK

LANES = 128
CHUNK = 64
VMEM_LIMIT = 56 * 1024 * 1024


def _cparams(*sem):
    return pltpu.CompilerParams(dimension_semantics=sem, vmem_limit_bytes=VMEM_LIMIT)


def _resident():
    return pl.BlockSpec(memory_space=pltpu.VMEM)


def _split2(x):
    hi = x.astype(BF16)
    lo = (x - hi.astype(F32)).astype(BF16)
    return hi, lo


def _head_sum(x, bd):
    hi, lo = _split2(x)
    tiles = []
    for j in range(x.shape[1] // LANES):
        sl = slice(j * LANES, (j + 1) * LANES)
        tiles.append(jnp.dot(hi[:, sl], bd, preferred_element_type=F32)
                     + jnp.dot(lo[:, sl], bd, preferred_element_type=F32))
    return jnp.concatenate(tiles, axis=1)


def _block_diag_ones():
    r = lax.broadcasted_iota(jnp.int32, (LANES, LANES), 0) // HEAD_DIM
    c = lax.broadcasted_iota(jnp.int32, (LANES, LANES), 1) // HEAD_DIM
    return jnp.where(r == c, 1.0, 0.0).astype(BF16)


def _inproj_kernel(x_ref, nw_ref, wz_ref, wq_ref, wkv_ref, z_ref, q_ref, kv_ref):
    x = x_ref[...]
    ms = jnp.mean(x * x, axis=-1, keepdims=True)
    h = (x * lax.rsqrt(ms + NORM_EPS) * nw_ref[...]).astype(BF16)
    z_ref[...] = jnp.dot(h, wz_ref[...], preferred_element_type=F32)
    q_ref[...] = jnp.dot(h, wq_ref[...], preferred_element_type=F32)
    kv_ref[...] = jnp.dot(h, wkv_ref[...], preferred_element_type=F32)


def _inproj(xf, norm_w, w_in, tm=256):
    T, D = xf.shape
    wz = w_in[:, :RWKV_COLS].astype(BF16)
    wq = w_in[:, RWKV_COLS:RWKV_COLS + ATTN_W].astype(BF16)
    wkv = w_in[:, RWKV_COLS + ATTN_W:].astype(BF16)
    row = lambda n: pl.BlockSpec((tm, n), lambda i: (i, 0))
    return pl.pallas_call(
        _inproj_kernel,
        grid=(T // tm,),
        in_specs=[row(D), _resident(), _resident(), _resident(), _resident()],
        out_specs=[row(RWKV_COLS), row(ATTN_W), row(2 * KV_W)],
        out_shape=[jax.ShapeDtypeStruct((T, RWKV_COLS), F32),
                   jax.ShapeDtypeStruct((T, ATTN_W), F32),
                   jax.ShapeDtypeStruct((T, 2 * KV_W), F32)],
        compiler_params=_cparams("parallel"),
        name="inproj",
    )(xf, norm_w.reshape(1, D), wz, wq, wkv)


def _rwkv_prep_kernel(z_ref, zp_ref, zn_ref, mup_ref, mun_ref, vec_ref,
                      wwf_ref, wwb_ref, waf_ref, wab_ref, wg_ref,
                      a_ref, b_ref, k_ref, r_ref, v_ref, g_ref, bonus_ref, wl_ref):
    i = pl.program_id(1)
    nblk = pl.num_programs(1)
    tb = z_ref.shape[1]
    z = z_ref[0]
    rows = lax.broadcasted_iota(jnp.int32, (tb, 1), 0)
    prev_row = jnp.where(i > 0, zp_ref[0, 7:8, :], 0.0)
    next_row = jnp.where(i < nblk - 1, zn_ref[0, 0:1, :], 0.0)
    zp = jnp.where(rows == 0, prev_row, pltpu.roll(z, 1, axis=0))
    zn = jnp.where(rows == tb - 1, next_row, pltpu.roll(z, tb - 1, axis=0))
    zs = z + mup_ref[...] * (zp - z) + mun_ref[...] * (zn - z)

    C = RWKV_W
    r = zs[:, 0:C]
    k = zs[:, C:2 * C]
    v = zs[:, 2 * C:3 * C]
    lora = zs[:, 3 * C:3 * C + LORA_W]
    w0f, w0b = vec_ref[0:1, :], vec_ref[1:2, :]
    a0f, a0b = vec_ref[2:3, :], vec_ref[3:4, :]
    k_k, k_a, r_k = vec_ref[4:5, :], vec_ref[5:6, :], vec_ref[6:7, :]

    th = jnp.tanh(lora[:, 0:256]).astype(BF16)
    al = lora[:, 128:384].astype(BF16)
    sg = jax.nn.sigmoid(lora[:, 384:640]).astype(BF16)
    dot = functools.partial(jnp.dot, preferred_element_type=F32)
    lw_f_in = w0f + dot(th[:, 0:128], wwf_ref[...])
    lw_b_in = w0b + dot(th, wwb_ref[...])
    a_f = jax.nn.sigmoid(a0f + dot(al, waf_ref[...]))
    a_b = jax.nn.sigmoid(a0b + dot(al[:, 128:256], wab_ref[...]))
    g_ref[0] = dot(sg, wg_ref[...])
    v_ref[0] = v

    def log_decay(y):
        sp = jnp.maximum(-y, 0.0) + jnp.log1p(jnp.exp(-jnp.abs(y)))
        return -jnp.exp(-sp - 0.5)

    bd = _block_diag_ones()
    kk = k * k_k
    nrm = jnp.sqrt(_head_sum(kk * kk, bd))
    kk = kk / jnp.maximum(nrm, 1e-12)

    tri_r = lax.broadcasted_iota(jnp.int32, (tb, tb), 0)
    tri_c = lax.broadcasted_iota(jnp.int32, (tb, tb), 1)
    same = (tri_r // CHUNK) == (tri_c // CHUNK)

    kdir_sum = jnp.zeros_like(k)
    nchunk = tb // CHUNK
    for d, (y, a) in enumerate(((lw_f_in, a_f), (lw_b_in, a_b))):
        lw = log_decay(y)
        keep = (tri_c <= tri_r) if d == 0 else (tri_c >= tri_r)
        tri = jnp.where(same & keep, 1.0, 0.0).astype(BF16)
        hi = lw.astype(BF16)
        r1 = lw - hi.astype(F32)
        mid = r1.astype(BF16)
        lo = (r1 - mid.astype(F32)).astype(BF16)
        cw = dot(tri, hi) + dot(tri, mid) + dot(tri, lo)
        w_incl = jnp.exp(cw)
        w_inv = jnp.exp(-cw)
        kdir = k * (1.0 + (a - 1.0) * k_a)
        kdir_sum = kdir_sum + kdir
        a_ref[d, 0] = -kk * jnp.exp(cw - lw)
        b_ref[d, 0] = kk * a * w_inv
        k_ref[d, 0] = kdir * w_inv
        r_ref[d, 0] = r * w_incl
        for c in range(nchunk):
            end = c * CHUNK + (CHUNK - 1 if d == 0 else 0)
            wl_ref[d, 0, c] = w_incl[end:end + 1, :]
    bonus_ref[0] = _head_sum(r * (0.5 * kdir_sum) * r_k, bd) * v


def _rwkv_prep(z, mu_prev, mu_next, w0, w2, a0, a2, g2, k_k, k_a, r_k, tb=256):
    B, S, _ = z.shape
    C = RWKV_W
    nc = S // CHUNK
    vec = jnp.concatenate([w0, a0, k_k.reshape(1, C), k_a.reshape(1, C), r_k.reshape(1, C),
                           jnp.zeros((1, C), F32)], axis=0)
    zpad = lambda rows_before, w, rows_total: jnp.pad(
        w, ((rows_before, rows_total - rows_before - w.shape[0]), (0, 0))).astype(BF16)
    wwf = zpad(0, w2[0], 128)
    wwb = zpad(96, w2[1], 256)
    waf = zpad(64, a2[0], 256)
    wab = zpad(32, a2[1], 128)
    wg = g2.astype(BF16)
    blk = lambda: pl.BlockSpec((1, tb, C), lambda b, i: (b, i, 0))
    dblk = lambda: pl.BlockSpec((2, 1, tb, C), lambda b, i: (0, b, i, 0))
    g8 = tb // 8
    n8 = S // 8
    return pl.pallas_call(
        _rwkv_prep_kernel,
        grid=(B, S // tb),
        in_specs=[pl.BlockSpec((1, tb, RWKV_COLS), lambda b, i: (b, i, 0)),
                  pl.BlockSpec((1, 8, RWKV_COLS), lambda b, i: (b, jnp.maximum(i * g8 - 1, 0), 0)),
                  pl.BlockSpec((1, 8, RWKV_COLS), lambda b, i: (b, jnp.minimum((i + 1) * g8, n8 - 1), 0)),
                  _resident(), _resident(), _resident(),
                  _resident(), _resident(), _resident(), _resident(), _resident()],
        out_specs=[dblk(), dblk(), dblk(), dblk(), blk(), blk(), blk(),
                   pl.BlockSpec((2, 1, tb // CHUNK, 1, C), lambda b, i: (0, b, i, 0, 0))],
        out_shape=[jax.ShapeDtypeStruct((2, B, S, C), F32)] * 4
                  + [jax.ShapeDtypeStruct((B, S, C), F32)] * 3
                  + [jax.ShapeDtypeStruct((2, B, nc, 1, C), F32)],
        compiler_params=_cparams("parallel", "parallel"),
        name="rwkv_prep",
    )(z, z, z, mu_prev.reshape(1, -1), mu_next.reshape(1, -1), vec, wwf, wwb, waf, wab, wg)


def _rwkv_scan_kernel(a_ref, b_ref, k_ref, r_ref, v_ref, wl_ref, y_ref, s_ref):
    d = pl.program_id(0)
    c = pl.program_id(2)

    @pl.when(c == 0)
    def _():
        s_ref[...] = jnp.zeros_like(s_ref)

    L = CHUNK
    N = HEAD_DIM
    row = lax.broadcasted_iota(jnp.int32, (L, L), 0)
    col = lax.broadcasted_iota(jnp.int32, (L, L), 1)
    lag = (row - col) * (1 - 2 * d)
    strict = lag > 0
    incl = lag >= 0
    dot = functools.partial(jnp.dot, preferred_element_type=F32)
    nt = lambda x, y: lax.dot_general(x, y, (((1,), (1,)), ((), ())), preferred_element_type=F32)
    tn = lambda x, y: lax.dot_general(x, y, (((0,), (0,)), ((), ())), preferred_element_type=F32)

    heads = range(RWKV_HEADS)
    sls = [slice(h * N, (h + 1) * N) for h in heads]
    V = [v_ref[0, :, sl] for sl in sls]
    S = [s_ref[h] for h in heads]
    AR = [jnp.concatenate([a_ref[0, 0, :, sl], r_ref[0, 0, :, sl]], axis=0) for sl in sls]
    BK = [jnp.concatenate([b_ref[0, 0, :, sl], k_ref[0, 0, :, sl]], axis=0) for sl in sls]
    nf = [nt(AR[h], BK[h]) for h in heads]
    ars = [nt(AR[h], S[h]) for h in heads]
    npow = [jnp.where(strict, nf[h][:L, :L], 0.0) for h in heads]
    p = [ars[h][:L] + dot(jnp.where(strict, nf[h][:L, L:], 0.0), V[h]) for h in heads]
    for it in range(6):
        p = [p[h] + dot(npow[h], p[h]) for h in heads]
        if it < 5:
            npow = [dot(npow[h], npow[h]) for h in heads]
    for h in heads:
        n_rb = jnp.where(incl, nf[h][L:, :L], 0.0)
        n_rk = jnp.where(incl, nf[h][L:, L:], 0.0)
        y_ref[0, 0, :, sls[h]] = ars[h][L:] + dot(n_rb, p[h]) + dot(n_rk, V[h])
    for h in heads:
        UV = jnp.concatenate([p[h], V[h]], axis=0)
        s_ref[h] = (S[h] + tn(UV, BK[h])) * wl_ref[0, 0, 0, :, sls[h]]


def _rwkv_scan(A, Bm, K, R, V, WL):
    _, B, S, C = A.shape
    nc = S // CHUNK
    cidx = lambda d, c: c + d * (nc - 1 - 2 * c)
    dspec = lambda: pl.BlockSpec((1, 1, CHUNK, C), lambda d, b, c: (d, b, cidx(d, c), 0))
    return pl.pallas_call(
        _rwkv_scan_kernel,
        grid=(2, B, nc),
        in_specs=[dspec(), dspec(), dspec(), dspec(),
                  pl.BlockSpec((1, CHUNK, C), lambda d, b, c: (b, cidx(d, c), 0)),
                  pl.BlockSpec((1, 1, 1, 1, C), lambda d, b, c: (d, b, cidx(d, c), 0, 0))],
        out_specs=dspec(),
        out_shape=jax.ShapeDtypeStruct((2, B, S, C), F32),
        scratch_shapes=[pltpu.VMEM((RWKV_HEADS, HEAD_DIM, HEAD_DIM), F32)],
        compiler_params=_cparams("parallel", "parallel", "arbitrary"),
        name="rwkv_scan",
    )(A, Bm, K, R, V, WL)


def _rope_tables(S):
    pos = jnp.arange(S)
    rows = (pos // GRID_W).astype(F32)
    cols = (pos % GRID_W).astype(F32)
    inv = ROPE_THETA ** (-jnp.arange(0, ROPE_AXIS_DIM, 2, dtype=F32) / ROPE_AXIS_DIM)
    ang_r = rows[:, None] * inv[None]
    ang_c = cols[:, None] * inv[None]
    zero = jnp.zeros_like(ang_r)
    cos_h = jnp.concatenate([jnp.cos(ang_r), jnp.cos(ang_r), jnp.cos(ang_c), jnp.cos(ang_c)], axis=1)
    sin_lo = jnp.concatenate([-jnp.sin(ang_r), zero, -jnp.sin(ang_c), zero], axis=1)
    sin_hi = jnp.concatenate([zero, jnp.sin(ang_r), zero, jnp.sin(ang_c)], axis=1)
    two = lambda t: jnp.concatenate([t, t], axis=1)
    return two(cos_h), two(sin_lo), two(sin_hi)


def _norm_rope(x, w, cos, sin_lo, sin_hi, bd, scale):
    ms = _head_sum(x * x, bd) * (1.0 / HEAD_DIM)
    tiles = []
    for j in range(x.shape[1] // LANES):
        sl = slice(j * LANES, (j + 1) * LANES)
        y = x[:, sl] * lax.rsqrt(ms[:, sl] + NORM_EPS) * w
        rot = (y * cos + pltpu.roll(y, LANES - 16, axis=1) * sin_lo + pltpu.roll(y, 16, axis=1) * sin_hi)
        tiles.append((rot * scale).astype(BF16))
    return jnp.concatenate(tiles, axis=1)


def _attn_prep_kernel(q_ref, kv_ref, qw_ref, kw_ref, cos_ref, slo_ref, shi_ref, qo_ref, ko_ref, vo_ref):
    bd = _block_diag_ones()
    cos, slo, shi = cos_ref[...], slo_ref[...], shi_ref[...]
    qo_ref[0] = _norm_rope(q_ref[0], qw_ref[...], cos, slo, shi, bd, HEAD_DIM ** -0.5)
    ko_ref[0] = _norm_rope(kv_ref[0, :, 0:KV_W], kw_ref[...], cos, slo, shi, bd, 1.0)
    vo_ref[0] = kv_ref[0, :, KV_W:2 * KV_W].astype(BF16)


def _attn_prep(q, kv, q_norm_w, k_norm_w, tb=256):
    B, S, _ = q.shape
    cos, slo, shi = _rope_tables(S)
    tile2 = lambda w: jnp.concatenate([w, w]).reshape(1, LANES)
    blk = lambda n: pl.BlockSpec((1, tb, n), lambda b, i: (b, i, 0))
    tab = lambda: pl.BlockSpec((tb, LANES), lambda b, i: (i, 0))
    return pl.pallas_call(
        _attn_prep_kernel,
        grid=(B, S // tb),
        in_specs=[blk(ATTN_W), blk(2 * KV_W), _resident(), _resident(), tab(), tab(), tab()],
        out_specs=[blk(ATTN_W), blk(KV_W), blk(KV_W)],
        out_shape=[jax.ShapeDtypeStruct((B, S, ATTN_W), BF16),
                   jax.ShapeDtypeStruct((B, S, KV_W), BF16),
                   jax.ShapeDtypeStruct((B, S, KV_W), BF16)],
        compiler_params=_cparams("parallel", "parallel"),
        name="attn_prep",
    )(q, kv, tile2(q_norm_w), tile2(k_norm_w), cos, slo, shi)


def _attention_kernel(q_ref, k_ref, v_ref, o_ref):
    tq = q_ref.shape[1]
    groups = ATTN_Q_HEADS // ATTN_KV_HEADS
    for kvh in range(2):
        ksl = slice(kvh * HEAD_DIM, (kvh + 1) * HEAD_DIM)
        k = k_ref[0, :, ksl]
        v = v_ref[0, :, ksl]
        qsl = [slice((kvh * groups + g) * HEAD_DIM, (kvh * groups + g + 1) * HEAD_DIM) for g in range(groups)]
        q4 = jnp.concatenate([q_ref[0, :, sl] for sl in qsl], axis=0)
        s = lax.dot_general(q4, k, (((1,), (1,)), ((), ())), preferred_element_type=F32)
        m = jnp.max(s, axis=-1, keepdims=True)
        p = jnp.exp(s - m)
        l = jnp.sum(p, axis=-1, keepdims=True)
        o = jnp.dot(p.astype(BF16), v, preferred_element_type=F32) / l
        for g in range(groups):
            o_ref[0, :, qsl[g]] = o[g * tq:(g + 1) * tq]


def _attention(q, k, v, tq=128):
    B, S, _ = q.shape
    wq = ATTN_W // 2
    return pl.pallas_call(
        _attention_kernel,
        grid=(B, 2, S // tq),
        in_specs=[pl.BlockSpec((1, tq, wq), lambda b, p, i: (b, i, p)),
                  pl.BlockSpec((1, S, LANES), lambda b, p, i: (b, 0, p)),
                  pl.BlockSpec((1, S, LANES), lambda b, p, i: (b, 0, p))],
        out_specs=pl.BlockSpec((1, tq, wq), lambda b, p, i: (b, i, p)),
        out_shape=jax.ShapeDtypeStruct((B, S, ATTN_W), F32),
        compiler_params=_cparams("parallel", "parallel", "parallel"),
        name="attention",
    )(q, k, v)


def _mix_out_kernel(y_ref, bonus_ref, g_ref, attn_ref, x_ref, wo1_ref, wo2_ref, lw_ref, lb_ref, n2_ref,
                    x1_ref, h2_ref):
    bd = _block_diag_ones()
    y = y_ref[0] + y_ref[1]
    mean = _head_sum(y, bd) * (1.0 / HEAD_DIM)
    dlt = y - mean
    var = _head_sum(dlt * dlt, bd) * (1.0 / HEAD_DIM)
    yn = dlt * lax.rsqrt(var + GN_EPS) * lw_ref[...] + lb_ref[...]
    yr = ((yn + bonus_ref[...]) * g_ref[...]).astype(BF16)
    x1 = (x_ref[...] + jnp.dot(yr, wo1_ref[...], preferred_element_type=F32)
          + jnp.dot(attn_ref[...].astype(BF16), wo2_ref[...], preferred_element_type=F32))
    x1_ref[...] = x1
    ms = jnp.mean(x1 * x1, axis=-1, keepdims=True)
    h2_ref[...] = x1 * lax.rsqrt(ms + NORM_EPS) * n2_ref[...]


def _mix_out(y2, bonus, g, attn, xf, w_out, lnx_w, lnx_b, norm2_w, tm=256):
    T, D = xf.shape
    C = RWKV_W
    wo1 = w_out[:C].astype(BF16)
    wo2 = w_out[C:].astype(BF16)
    row = lambda n: pl.BlockSpec((tm, n), lambda i: (i, 0))
    return pl.pallas_call(
        _mix_out_kernel,
        grid=(T // tm,),
        in_specs=[pl.BlockSpec((2, tm, C), lambda i: (0, i, 0)), row(C), row(C), row(ATTN_W), row(D),
                  _resident(), _resident(), _resident(), _resident(), _resident()],
        out_specs=[row(D), row(D)],
        out_shape=[jax.ShapeDtypeStruct((T, D), F32), jax.ShapeDtypeStruct((T, D), F32)],
        compiler_params=_cparams("parallel"),
        name="mix_out",
    )(y2, bonus, g, attn, xf, wo1, wo2, lnx_w.reshape(1, C), lnx_b.reshape(1, C), norm2_w.reshape(1, D))


def _top16(vals, ids):
    big = jnp.int32(1 << 30)
    out_v, out_i = [], []
    cur = vals
    for _ in range(PEER_TOPK):
        m = jnp.max(cur, axis=0, keepdims=True)
        pick = jnp.min(jnp.where(cur == m, ids, big), axis=0, keepdims=True)
        out_v.append(m)
        out_i.append(pick)
        cur = jnp.where(ids == pick, -jnp.inf, cur)
    return out_v, out_i


def _peer_route_kernel(h_ref, wpq_ref, sk_ref, ids_ref, gate_ref):
    tb = h_ref.shape[0]
    q = jnp.dot(h_ref[...].astype(BF16), wpq_ref[...], preferred_element_type=F32)
    key_id = lax.broadcasted_iota(jnp.int32, (N_KEYS, tb), 0)
    flat_id = lax.broadcasted_iota(jnp.int32, (PEER_TOPK * PEER_TOPK, tb), 0)
    id_rows, gate_rows = [], []
    for h in range(PEER_HEADS):
        sv, si = [], []
        for p in range(2):
            hp = 2 * h + p
            qs = q[:, hp * PEER_HALF:(hp + 1) * PEER_HALF].astype(BF16)
            st = lax.dot_general(sk_ref[hp], qs, (((1,), (1,)), ((), ())),
                                 preferred_element_type=F32)
            v, i = _top16(st, key_id)
            sv.append(v)
            si.append(i)
        s2 = jnp.concatenate(sv[1], axis=0)
        i2 = jnp.concatenate(si[1], axis=0)
        cand = jnp.concatenate([sv[0][a] + s2 for a in range(PEER_TOPK)], axis=0)
        cexp = jnp.concatenate([si[0][a] * N_KEYS + i2 for a in range(PEER_TOPK)], axis=0)
        best, pick = _top16(cand, flat_id)
        top = best[0]
        ex = [jnp.exp(b - top) for b in best]
        den = ex[0]
        for e in ex[1:]:
            den = den + e
        for j in range(PEER_TOPK):
            gate_rows.append(ex[j] / den)
            id_rows.append(jnp.max(jnp.where(flat_id == pick[j], cexp, -1), axis=0, keepdims=True))
    gates = jnp.concatenate(gate_rows, axis=0)
    ids = jnp.concatenate(id_rows, axis=0)
    ids_ref[...] = ids.T
    r = lax.broadcasted_iota(jnp.int32, (PEER_SLOTS, 2 * PEER_SLOTS), 0)
    c = lax.broadcasted_iota(jnp.int32, (PEER_SLOTS, 2 * PEER_SLOTS), 1)
    spread = jnp.where(c == 2 * r, 1.0, 0.0).astype(BF16)
    ghi, glo = _split2(gates)
    tn = lambda a, b: lax.dot_general(a, b, (((0,), (0,)), ((), ())), preferred_element_type=F32)
    gate_ref[...] = tn(ghi, spread) + tn(glo, spread)


def _peer_route(h2, w_pq, sub_keys, tb=256):
    T, D = h2.shape
    wpq = w_pq.astype(BF16)
    sk = sub_keys.reshape(PEER_HEADS * 2, N_KEYS, PEER_HALF).astype(BF16)
    return pl.pallas_call(
        _peer_route_kernel,
        grid=(T // tb,),
        in_specs=[pl.BlockSpec((tb, D), lambda i: (i, 0)), _resident(), _resident()],
        out_specs=[pl.BlockSpec((tb, PEER_SLOTS), lambda i: (i, 0)),
                   pl.BlockSpec((tb, 2 * PEER_SLOTS), lambda i: (i, 0))],
        out_shape=[jax.ShapeDtypeStruct((T, PEER_SLOTS), jnp.int32),
                   jax.ShapeDtypeStruct((T, 2 * PEER_SLOTS), F32)],
        compiler_params=_cparams("parallel"),
        name="peer_route",
    )(h2, wpq, sk)


PEER_TB = 32
PEER_NBUF = 8
PEER_AHEAD = 6


def _peer_expert_kernel(ids_ref, idn_ref, tab_ref, gate_ref, h_ref, x1_ref, nf_ref, o_ref, buf, sem, *,
                        final_norm):
    i = pl.program_id(0)
    nblk = pl.num_programs(0)
    tb = h_ref.shape[0]
    K = PEER_SLOTS

    def row_copy(ids, t, k, slot):
        return pltpu.make_async_copy(tab_ref.at[ids[t, k]], buf.at[slot, pl.ds(2 * k, 2), :], sem.at[slot])

    def gather(ids, t, slot):
        def body(k8, c):
            for u in range(8):
                row_copy(ids, t, k8 * 8 + u, slot).start()
            return c
        lax.fori_loop(0, K // 8, body, 0)

    @pl.when(i == 0)
    def _():
        for t in range(PEER_AHEAD):
            gather(ids_ref, t, t % PEER_NBUF)

    def token(j, c):
        g = i * tb + j
        ahead = j + PEER_AHEAD
        aslot = (g + PEER_AHEAD) % PEER_NBUF

        @pl.when(ahead < tb)
        def _():
            gather(ids_ref, ahead, aslot)

        @pl.when((ahead >= tb) & (i < nblk - 1))
        def _():
            gather(idn_ref, ahead - tb, aslot)

        slot = g % PEER_NBUF
        pltpu.make_async_copy(buf.at[slot], buf.at[slot], sem.at[slot]).wait()
        w = buf[slot]
        h = h_ref[pl.ds(j, 1), :].astype(BF16)
        s = lax.dot_general(h, w, (((1,), (1,)), ((), ())), preferred_element_type=F32)
        act = 0.5 * s * (1.0 + lax.erf(s * (2.0 ** -0.5))) * gate_ref[pl.ds(j, 1), :]
        act = pltpu.roll(act, 1, axis=1).astype(BF16)
        x2 = x1_ref[pl.ds(j, 1), :] + jnp.dot(act, w, preferred_element_type=F32)
        if final_norm:
            ms = jnp.mean(x2 * x2, axis=-1, keepdims=True)
            x2 = x2 * lax.rsqrt(ms + NORM_EPS) * nf_ref[...]
        o_ref[pl.ds(j, 1), :] = x2
        return c

    lax.fori_loop(0, tb, token, 0)


def _pack_tables(u_tab, v_tab):
    return jnp.stack([u_tab.astype(BF16), v_tab.astype(BF16)], axis=1)


def _peer_expert(ids, gate2, h2, x1, u_tab, v_tab, normf_w, final_norm):
    T, D = x1.shape
    tb = PEER_TB
    nblk = T // tb
    tab = _pack_tables(u_tab, v_tab)
    row = lambda n: pl.BlockSpec((tb, n), lambda i: (i, 0))
    return pl.pallas_call(
        functools.partial(_peer_expert_kernel, final_norm=final_norm),
        grid=(nblk,),
        in_specs=[pl.BlockSpec((tb, PEER_SLOTS), lambda i: (i, 0), memory_space=pltpu.SMEM),
                  pl.BlockSpec((tb, PEER_SLOTS), lambda i: (jnp.minimum(i + 1, nblk - 1), 0),
                               memory_space=pltpu.SMEM),
                  pl.BlockSpec(memory_space=pl.ANY),
                  row(2 * PEER_SLOTS), row(D), row(D), _resident()],
        out_specs=row(D),
        out_shape=jax.ShapeDtypeStruct((T, D), F32),
        scratch_shapes=[pltpu.VMEM((PEER_NBUF, 2 * PEER_SLOTS, D), BF16),
                        pltpu.SemaphoreType.DMA((PEER_NBUF,))],
        compiler_params=_cparams("arbitrary"),
        name="peer_expert",
    )(ids, ids, tab, gate2, h2, x1, normf_w.reshape(1, D))


def kernel(x, w_in, mu_prev, mu_next, w0, w2, a0, a2, g2, k_k, k_a, r_k, lnx_w, lnx_b, q_norm_w, k_norm_w, w_out, norm1_w, norm2_w, w_pq, sub_keys, u_tab, v_tab, normf_w):
    B, S, D = x.shape
    depth = w_in.shape[0]
    xf = x.reshape(B * S, D)
    for l in range(depth):
        z, q, kv = _inproj(xf, norm1_w[l], w_in[l])
        A, Bm, K, R, V, G, bonus, WL = _rwkv_prep(
            z.reshape(B, S, RWKV_COLS), mu_prev[l], mu_next[l], w0[l], w2[l], a0[l], a2[l], g2[l],
            k_k[l], k_a[l], r_k[l])
        y2 = _rwkv_scan(A, Bm, K, R, V, WL)
        qr, kr, vr = _attn_prep(q.reshape(B, S, ATTN_W), kv.reshape(B, S, 2 * KV_W), q_norm_w[l], k_norm_w[l])
        attn = _attention(qr, kr, vr)
        x1, h2 = _mix_out(y2.reshape(2, B * S, RWKV_W), bonus.reshape(B * S, RWKV_W), G.reshape(B * S, RWKV_W),
                          attn.reshape(B * S, ATTN_W), xf, w_out[l], lnx_w[l], lnx_b[l], norm2_w[l])
        ids, gate2 = _peer_route(h2, w_pq[l], sub_keys[l])
        xf = _peer_expert(ids, gate2, h2, x1, u_tab[l], v_tab[l], normf_w, final_norm=(l == depth - 1))
    return xf.reshape(B, S, D)
```

```python
import functools

import jax
import jax.numpy as jnp
from jax import lax
from jax.experimental import pallas as pl
from jax.experimental.pallas import tpu as pltpu

F32 = jnp.float32
BF16 = jnp.bfloat16

HEAD_DIM = 64
RWKV_W = 1024
RWKV_HEADS = 16
DECAY_LORA = 96
AAA_LORA = 96
GATE_LORA = 256
LORA_W = 2 * DECAY_LORA + 2 * AAA_LORA + GATE_LORA
RWKV_COLS = 3 * RWKV_W + LORA_W
ATTN_W = 1024
KV_W = 256
ATTN_Q_HEADS = 16
ATTN_KV_HEADS = 4
GRID_W = 64
ROPE_THETA = 10000.0
ROPE_AXIS_DIM = 32
GN_EPS = 64e-5
NORM_EPS = 1e-6
PEER_HEADS = 8
PEER_HALF = 128
N_KEYS = 128
PEER_TOPK = 16
PEER_SLOTS = PEER_HEADS * PEER_TOPK

LANES = 128
CHUNK = 64
VMEM_LIMIT = 56 * 1024 * 1024


def _cparams(*sem):
    return pltpu.CompilerParams(dimension_semantics=sem, vmem_limit_bytes=VMEM_LIMIT)


def _resident():
    return pl.BlockSpec(memory_space=pltpu.VMEM)


def _split2(x):
    hi = x.astype(BF16)
    lo = (x - hi.astype(F32)).astype(BF16)
    return hi, lo


def _head_sum(x, bd):
    hi, lo = _split2(x)
    tiles = []
    for j in range(x.shape[1] // LANES):
        sl = slice(j * LANES, (j + 1) * LANES)
        tiles.append(jnp.dot(hi[:, sl], bd, preferred_element_type=F32)
                     + jnp.dot(lo[:, sl], bd, preferred_element_type=F32))
    return jnp.concatenate(tiles, axis=1)


def _block_diag_ones():
    r = lax.broadcasted_iota(jnp.int32, (LANES, LANES), 0) // HEAD_DIM
    c = lax.broadcasted_iota(jnp.int32, (LANES, LANES), 1) // HEAD_DIM
    return jnp.where(r == c, 1.0, 0.0).astype(BF16)


def _inproj_kernel(x_ref, nw_ref, wz_ref, wq_ref, wkv_ref, z_ref, q_ref, kv_ref):
    x = x_ref[...]
    ms = jnp.mean(x * x, axis=-1, keepdims=True)
    h = (x * lax.rsqrt(ms + NORM_EPS) * nw_ref[...]).astype(BF16)
    z_ref[...] = jnp.dot(h, wz_ref[...], preferred_element_type=F32)
    q_ref[...] = jnp.dot(h, wq_ref[...], preferred_element_type=F32)
    kv_ref[...] = jnp.dot(h, wkv_ref[...], preferred_element_type=F32)


def _inproj(xf, norm_w, w_in, tm=256):
    T, D = xf.shape
    wz = w_in[:, :RWKV_COLS].astype(BF16)
    wq = w_in[:, RWKV_COLS:RWKV_COLS + ATTN_W].astype(BF16)
    wkv = w_in[:, RWKV_COLS + ATTN_W:].astype(BF16)
    row = lambda n: pl.BlockSpec((tm, n), lambda i: (i, 0))
    return pl.pallas_call(
        _inproj_kernel,
        grid=(T // tm,),
        in_specs=[row(D), _resident(), _resident(), _resident(), _resident()],
        out_specs=[row(RWKV_COLS), row(ATTN_W), row(2 * KV_W)],
        out_shape=[jax.ShapeDtypeStruct((T, RWKV_COLS), F32),
                   jax.ShapeDtypeStruct((T, ATTN_W), F32),
                   jax.ShapeDtypeStruct((T, 2 * KV_W), F32)],
        compiler_params=_cparams("parallel"),
        name="inproj",
    )(xf, norm_w.reshape(1, D), wz, wq, wkv)


def _rwkv_prep_kernel(z_ref, zp_ref, zn_ref, mup_ref, mun_ref, vec_ref,
                      wwf_ref, wwb_ref, waf_ref, wab_ref, wg_ref,
                      a_ref, b_ref, k_ref, r_ref, v_ref, g_ref, bonus_ref, wl_ref):
    i = pl.program_id(1)
    nblk = pl.num_programs(1)
    tb = z_ref.shape[1]
    z = z_ref[0]
    rows = lax.broadcasted_iota(jnp.int32, (tb, 1), 0)
    prev_row = jnp.where(i > 0, zp_ref[0, 7:8, :], 0.0)
    next_row = jnp.where(i < nblk - 1, zn_ref[0, 0:1, :], 0.0)
    zp = jnp.where(rows == 0, prev_row, pltpu.roll(z, 1, axis=0))
    zn = jnp.where(rows == tb - 1, next_row, pltpu.roll(z, tb - 1, axis=0))
    zs = z + mup_ref[...] * (zp - z) + mun_ref[...] * (zn - z)

    C = RWKV_W
    r = zs[:, 0:C]
    k = zs[:, C:2 * C]
    v = zs[:, 2 * C:3 * C]
    lora = zs[:, 3 * C:3 * C + LORA_W]
    w0f, w0b = vec_ref[0:1, :], vec_ref[1:2, :]
    a0f, a0b = vec_ref[2:3, :], vec_ref[3:4, :]
    k_k, k_a, r_k = vec_ref[4:5, :], vec_ref[5:6, :], vec_ref[6:7, :]

    th = jnp.tanh(lora[:, 0:256]).astype(BF16)
    al = lora[:, 128:384].astype(BF16)
    sg = jax.nn.sigmoid(lora[:, 384:640]).astype(BF16)
    dot = functools.partial(jnp.dot, preferred_element_type=F32)
    lw_f_in = w0f + dot(th[:, 0:128], wwf_ref[...])
    lw_b_in = w0b + dot(th, wwb_ref[...])
    a_f = jax.nn.sigmoid(a0f + dot(al, waf_ref[...]))
    a_b = jax.nn.sigmoid(a0b + dot(al[:, 128:256], wab_ref[...]))
    g_ref[0] = dot(sg, wg_ref[...])
    v_ref[0] = v

    def log_decay(y):
        sp = jnp.maximum(-y, 0.0) + jnp.log1p(jnp.exp(-jnp.abs(y)))
        return -jnp.exp(-sp - 0.5)

    bd = _block_diag_ones()
    kk = k * k_k
    nrm = jnp.sqrt(_head_sum(kk * kk, bd))
    kk = kk / jnp.maximum(nrm, 1e-12)

    tri_r = lax.broadcasted_iota(jnp.int32, (tb, tb), 0)
    tri_c = lax.broadcasted_iota(jnp.int32, (tb, tb), 1)
    same = (tri_r // CHUNK) == (tri_c // CHUNK)

    kdir_sum = jnp.zeros_like(k)
    nchunk = tb // CHUNK
    for d, (y, a) in enumerate(((lw_f_in, a_f), (lw_b_in, a_b))):
        lw = log_decay(y)
        keep = (tri_c <= tri_r) if d == 0 else (tri_c >= tri_r)
        tri = jnp.where(same & keep, 1.0, 0.0).astype(BF16)
        hi = lw.astype(BF16)
        r1 = lw - hi.astype(F32)
        mid = r1.astype(BF16)
        lo = (r1 - mid.astype(F32)).astype(BF16)
        cw = dot(tri, hi) + dot(tri, mid) + dot(tri, lo)
        w_incl = jnp.exp(cw)
        w_inv = jnp.exp(-cw)
        kdir = k * (1.0 + (a - 1.0) * k_a)
        kdir_sum = kdir_sum + kdir
        a_ref[d, 0] = -kk * jnp.exp(cw - lw)
        b_ref[d, 0] = kk * a * w_inv
        k_ref[d, 0] = kdir * w_inv
        r_ref[d, 0] = r * w_incl
        for c in range(nchunk):
            end = c * CHUNK + (CHUNK - 1 if d == 0 else 0)
            wl_ref[d, 0, c] = w_incl[end:end + 1, :]
    bonus_ref[0] = _head_sum(r * (0.5 * kdir_sum) * r_k, bd) * v


def _rwkv_prep(z, mu_prev, mu_next, w0, w2, a0, a2, g2, k_k, k_a, r_k, tb=256):
    B, S, _ = z.shape
    C = RWKV_W
    nc = S // CHUNK
    vec = jnp.concatenate([w0, a0, k_k.reshape(1, C), k_a.reshape(1, C), r_k.reshape(1, C),
                           jnp.zeros((1, C), F32)], axis=0)
    zpad = lambda rows_before, w, rows_total: jnp.pad(
        w, ((rows_before, rows_total - rows_before - w.shape[0]), (0, 0))).astype(BF16)
    wwf = zpad(0, w2[0], 128)
    wwb = zpad(96, w2[1], 256)
    waf = zpad(64, a2[0], 256)
    wab = zpad(32, a2[1], 128)
    wg = g2.astype(BF16)
    blk = lambda: pl.BlockSpec((1, tb, C), lambda b, i: (b, i, 0))
    dblk = lambda: pl.BlockSpec((2, 1, tb, C), lambda b, i: (0, b, i, 0))
    g8 = tb // 8
    n8 = S // 8
    return pl.pallas_call(
        _rwkv_prep_kernel,
        grid=(B, S // tb),
        in_specs=[pl.BlockSpec((1, tb, RWKV_COLS), lambda b, i: (b, i, 0)),
                  pl.BlockSpec((1, 8, RWKV_COLS), lambda b, i: (b, jnp.maximum(i * g8 - 1, 0), 0)),
                  pl.BlockSpec((1, 8, RWKV_COLS), lambda b, i: (b, jnp.minimum((i + 1) * g8, n8 - 1), 0)),
                  _resident(), _resident(), _resident(),
                  _resident(), _resident(), _resident(), _resident(), _resident()],
        out_specs=[dblk(), dblk(), dblk(), dblk(), blk(), blk(), blk(),
                   pl.BlockSpec((2, 1, tb // CHUNK, 1, C), lambda b, i: (0, b, i, 0, 0))],
        out_shape=[jax.ShapeDtypeStruct((2, B, S, C), F32)] * 4
                  + [jax.ShapeDtypeStruct((B, S, C), F32)] * 3
                  + [jax.ShapeDtypeStruct((2, B, nc, 1, C), F32)],
        compiler_params=_cparams("parallel", "parallel"),
        name="rwkv_prep",
    )(z, z, z, mu_prev.reshape(1, -1), mu_next.reshape(1, -1), vec, wwf, wwb, waf, wab, wg)


def _rwkv_scan_kernel(a_ref, b_ref, k_ref, r_ref, v_ref, wl_ref, y_ref, s_ref):
    d = pl.program_id(0)
    c = pl.program_id(2)

    @pl.when(c == 0)
    def _():
        s_ref[...] = jnp.zeros_like(s_ref)

    L = CHUNK
    N = HEAD_DIM
    row = lax.broadcasted_iota(jnp.int32, (L, L), 0)
    col = lax.broadcasted_iota(jnp.int32, (L, L), 1)
    lag = (row - col) * (1 - 2 * d)
    strict = lag > 0
    incl = lag >= 0
    dot = functools.partial(jnp.dot, preferred_element_type=F32)
    nt = lambda x, y: lax.dot_general(x, y, (((1,), (1,)), ((), ())), preferred_element_type=F32)
    tn = lambda x, y: lax.dot_general(x, y, (((0,), (0,)), ((), ())), preferred_element_type=F32)

    heads = range(RWKV_HEADS)
    sls = [slice(h * N, (h + 1) * N) for h in heads]
    V = [v_ref[0, :, sl] for sl in sls]
    S = [s_ref[h] for h in heads]
    AR = [jnp.concatenate([a_ref[0, 0, :, sl], r_ref[0, 0, :, sl]], axis=0) for sl in sls]
    BK = [jnp.concatenate([b_ref[0, 0, :, sl], k_ref[0, 0, :, sl]], axis=0) for sl in sls]
    nf = [nt(AR[h], BK[h]) for h in heads]
    ars = [nt(AR[h], S[h]) for h in heads]
    npow = [jnp.where(strict, nf[h][:L, :L], 0.0) for h in heads]
    p = [ars[h][:L] + dot(jnp.where(strict, nf[h][:L, L:], 0.0), V[h]) for h in heads]
    for it in range(6):
        p = [p[h] + dot(npow[h], p[h]) for h in heads]
        if it < 5:
            npow = [dot(npow[h], npow[h]) for h in heads]
    for h in heads:
        n_rb = jnp.where(incl, nf[h][L:, :L], 0.0)
        n_rk = jnp.where(incl, nf[h][L:, L:], 0.0)
        y_ref[0, 0, :, sls[h]] = ars[h][L:] + dot(n_rb, p[h]) + dot(n_rk, V[h])
    for h in heads:
        UV = jnp.concatenate([p[h], V[h]], axis=0)
        s_ref[h] = (S[h] + tn(UV, BK[h])) * wl_ref[0, 0, 0, :, sls[h]]


def _rwkv_scan(A, Bm, K, R, V, WL):
    _, B, S, C = A.shape
    nc = S // CHUNK
    cidx = lambda d, c: c + d * (nc - 1 - 2 * c)
    dspec = lambda: pl.BlockSpec((1, 1, CHUNK, C), lambda d, b, c: (d, b, cidx(d, c), 0))
    return pl.pallas_call(
        _rwkv_scan_kernel,
        grid=(2, B, nc),
        in_specs=[dspec(), dspec(), dspec(), dspec(),
                  pl.BlockSpec((1, CHUNK, C), lambda d, b, c: (b, cidx(d, c), 0)),
                  pl.BlockSpec((1, 1, 1, 1, C), lambda d, b, c: (d, b, cidx(d, c), 0, 0))],
        out_specs=dspec(),
        out_shape=jax.ShapeDtypeStruct((2, B, S, C), F32),
        scratch_shapes=[pltpu.VMEM((RWKV_HEADS, HEAD_DIM, HEAD_DIM), F32)],
        compiler_params=_cparams("parallel", "parallel", "arbitrary"),
        name="rwkv_scan",
    )(A, Bm, K, R, V, WL)


def _rope_tables(S):
    pos = jnp.arange(S)
    rows = (pos // GRID_W).astype(F32)
    cols = (pos % GRID_W).astype(F32)
    inv = ROPE_THETA ** (-jnp.arange(0, ROPE_AXIS_DIM, 2, dtype=F32) / ROPE_AXIS_DIM)
    ang_r = rows[:, None] * inv[None]
    ang_c = cols[:, None] * inv[None]
    zero = jnp.zeros_like(ang_r)
    cos_h = jnp.concatenate([jnp.cos(ang_r), jnp.cos(ang_r), jnp.cos(ang_c), jnp.cos(ang_c)], axis=1)
    sin_lo = jnp.concatenate([-jnp.sin(ang_r), zero, -jnp.sin(ang_c), zero], axis=1)
    sin_hi = jnp.concatenate([zero, jnp.sin(ang_r), zero, jnp.sin(ang_c)], axis=1)
    two = lambda t: jnp.concatenate([t, t], axis=1)
    return two(cos_h), two(sin_lo), two(sin_hi)


def _norm_rope(x, w, cos, sin_lo, sin_hi, bd, scale):
    ms = _head_sum(x * x, bd) * (1.0 / HEAD_DIM)
    tiles = []
    for j in range(x.shape[1] // LANES):
        sl = slice(j * LANES, (j + 1) * LANES)
        y = x[:, sl] * lax.rsqrt(ms[:, sl] + NORM_EPS) * w
        rot = (y * cos + pltpu.roll(y, LANES - 16, axis=1) * sin_lo + pltpu.roll(y, 16, axis=1) * sin_hi)
        tiles.append((rot * scale).astype(BF16))
    return jnp.concatenate(tiles, axis=1)


def _attn_prep_kernel(q_ref, kv_ref, qw_ref, kw_ref, cos_ref, slo_ref, shi_ref, qo_ref, ko_ref, vo_ref):
    bd = _block_diag_ones()
    cos, slo, shi = cos_ref[...], slo_ref[...], shi_ref[...]
    qo_ref[0] = _norm_rope(q_ref[0], qw_ref[...], cos, slo, shi, bd, HEAD_DIM ** -0.5)
    ko_ref[0] = _norm_rope(kv_ref[0, :, 0:KV_W], kw_ref[...], cos, slo, shi, bd, 1.0)
    vo_ref[0] = kv_ref[0, :, KV_W:2 * KV_W].astype(BF16)


def _attn_prep(q, kv, q_norm_w, k_norm_w, tb=256):
    B, S, _ = q.shape
    cos, slo, shi = _rope_tables(S)
    tile2 = lambda w: jnp.concatenate([w, w]).reshape(1, LANES)
    blk = lambda n: pl.BlockSpec((1, tb, n), lambda b, i: (b, i, 0))
    tab = lambda: pl.BlockSpec((tb, LANES), lambda b, i: (i, 0))
    return pl.pallas_call(
        _attn_prep_kernel,
        grid=(B, S // tb),
        in_specs=[blk(ATTN_W), blk(2 * KV_W), _resident(), _resident(), tab(), tab(), tab()],
        out_specs=[blk(ATTN_W), blk(KV_W), blk(KV_W)],
        out_shape=[jax.ShapeDtypeStruct((B, S, ATTN_W), BF16),
                   jax.ShapeDtypeStruct((B, S, KV_W), BF16),
                   jax.ShapeDtypeStruct((B, S, KV_W), BF16)],
        compiler_params=_cparams("parallel", "parallel"),
        name="attn_prep",
    )(q, kv, tile2(q_norm_w), tile2(k_norm_w), cos, slo, shi)


def _attention_kernel(q_ref, k_ref, v_ref, o_ref):
    tq = q_ref.shape[1]
    groups = ATTN_Q_HEADS // ATTN_KV_HEADS
    for kvh in range(2):
        ksl = slice(kvh * HEAD_DIM, (kvh + 1) * HEAD_DIM)
        k = k_ref[0, :, ksl]
        v = v_ref[0, :, ksl]
        qsl = [slice((kvh * groups + g) * HEAD_DIM, (kvh * groups + g + 1) * HEAD_DIM) for g in range(groups)]
        q4 = jnp.concatenate([q_ref[0, :, sl] for sl in qsl], axis=0)
        s = lax.dot_general(q4, k, (((1,), (1,)), ((), ())), preferred_element_type=F32)
        m = jnp.max(s, axis=-1, keepdims=True)
        p = jnp.exp(s - m)
        l = jnp.sum(p, axis=-1, keepdims=True)
        o = jnp.dot(p.astype(BF16), v, preferred_element_type=F32) / l
        for g in range(groups):
            o_ref[0, :, qsl[g]] = o[g * tq:(g + 1) * tq]


def _attention(q, k, v, tq=128):
    B, S, _ = q.shape
    wq = ATTN_W // 2
    return pl.pallas_call(
        _attention_kernel,
        grid=(B, 2, S // tq),
        in_specs=[pl.BlockSpec((1, tq, wq), lambda b, p, i: (b, i, p)),
                  pl.BlockSpec((1, S, LANES), lambda b, p, i: (b, 0, p)),
                  pl.BlockSpec((1, S, LANES), lambda b, p, i: (b, 0, p))],
        out_specs=pl.BlockSpec((1, tq, wq), lambda b, p, i: (b, i, p)),
        out_shape=jax.ShapeDtypeStruct((B, S, ATTN_W), F32),
        compiler_params=_cparams("parallel", "parallel", "parallel"),
        name="attention",
    )(q, k, v)


def _mix_out_kernel(y_ref, bonus_ref, g_ref, attn_ref, x_ref, wo1_ref, wo2_ref, lw_ref, lb_ref, n2_ref,
                    x1_ref, h2_ref):
    bd = _block_diag_ones()
    y = y_ref[0] + y_ref[1]
    mean = _head_sum(y, bd) * (1.0 / HEAD_DIM)
    dlt = y - mean
    var = _head_sum(dlt * dlt, bd) * (1.0 / HEAD_DIM)
    yn = dlt * lax.rsqrt(var + GN_EPS) * lw_ref[...] + lb_ref[...]
    yr = ((yn + bonus_ref[...]) * g_ref[...]).astype(BF16)
    x1 = (x_ref[...] + jnp.dot(yr, wo1_ref[...], preferred_element_type=F32)
          + jnp.dot(attn_ref[...].astype(BF16), wo2_ref[...], preferred_element_type=F32))
    x1_ref[...] = x1
    ms = jnp.mean(x1 * x1, axis=-1, keepdims=True)
    h2_ref[...] = x1 * lax.rsqrt(ms + NORM_EPS) * n2_ref[...]


def _mix_out(y2, bonus, g, attn, xf, w_out, lnx_w, lnx_b, norm2_w, tm=256):
    T, D = xf.shape
    C = RWKV_W
    wo1 = w_out[:C].astype(BF16)
    wo2 = w_out[C:].astype(BF16)
    row = lambda n: pl.BlockSpec((tm, n), lambda i: (i, 0))
    return pl.pallas_call(
        _mix_out_kernel,
        grid=(T // tm,),
        in_specs=[pl.BlockSpec((2, tm, C), lambda i: (0, i, 0)), row(C), row(C), row(ATTN_W), row(D),
                  _resident(), _resident(), _resident(), _resident(), _resident()],
        out_specs=[row(D), row(D)],
        out_shape=[jax.ShapeDtypeStruct((T, D), F32), jax.ShapeDtypeStruct((T, D), F32)],
        compiler_params=_cparams("parallel"),
        name="mix_out",
    )(y2, bonus, g, attn, xf, wo1, wo2, lnx_w.reshape(1, C), lnx_b.reshape(1, C), norm2_w.reshape(1, D))


def _top16(vals, ids):
    big = jnp.int32(1 << 30)
    out_v, out_i = [], []
    cur = vals
    for _ in range(PEER_TOPK):
        m = jnp.max(cur, axis=0, keepdims=True)
        pick = jnp.min(jnp.where(cur == m, ids, big), axis=0, keepdims=True)
        out_v.append(m)
        out_i.append(pick)
        cur = jnp.where(ids == pick, -jnp.inf, cur)
    return out_v, out_i


def _peer_route_kernel(h_ref, wpq_ref, sk_ref, ids_ref, gate_ref):
    tb = h_ref.shape[0]
    q = jnp.dot(h_ref[...].astype(BF16), wpq_ref[...], preferred_element_type=F32)
    key_id = lax.broadcasted_iota(jnp.int32, (N_KEYS, tb), 0)
    rr = lax.broadcasted_iota(jnp.int32, (80, tb), 0)
    a_of = jnp.where(rr < 16, 0, jnp.where(rr < 72, (rr - 8) // 8, rr - 64))
    b_of = jnp.where(rr < 16, rr, jnp.where(rr < 72, (rr - 8) % 8, 0))
    flat_id = a_of * PEER_TOPK + b_of
    pair_ok = (a_of + 1) * (b_of + 1) <= PEER_TOPK
    id_rows, gate_rows = [], []
    for h in range(PEER_HEADS):
        sv, si = [], []
        for p in range(2):
            hp = 2 * h + p
            qs = q[:, hp * PEER_HALF:(hp + 1) * PEER_HALF].astype(BF16)
            st = lax.dot_general(sk_ref[hp], qs, (((1,), (1,)), ((), ())),
                                 preferred_element_type=F32)
            v, i = _top16(st, key_id)
            sv.append(v)
            si.append(i)
        s1 = jnp.concatenate(sv[0], axis=0)
        s2 = jnp.concatenate(sv[1], axis=0)
        i1 = jnp.concatenate(si[0], axis=0)
        i2 = jnp.concatenate(si[1], axis=0)
        cand = jnp.concatenate([s1[0:1] + s2] + [s1[a:a + 1] + s2[0:8] for a in range(1, 8)]
                               + [s1[8:16] + s2[0:1]], axis=0)
        cexp = jnp.concatenate([i1[0:1] * N_KEYS + i2] + [i1[a:a + 1] * N_KEYS + i2[0:8] for a in range(1, 8)]
                               + [i1[8:16] * N_KEYS + i2[0:1]], axis=0)
        cand = jnp.where(pair_ok, cand, -jnp.inf)
        best, pick = _top16(cand, flat_id)
        top = best[0]
        ex = [jnp.exp(b - top) for b in best]
        den = ex[0]
        for e in ex[1:]:
            den = den + e
        for j in range(PEER_TOPK):
            gate_rows.append(ex[j] / den)
            id_rows.append(jnp.max(jnp.where(flat_id == pick[j], cexp, -1), axis=0, keepdims=True))
    gates = jnp.concatenate(gate_rows, axis=0)
    ids = jnp.concatenate(id_rows, axis=0)
    ids_ref[...] = ids.T
    r = lax.broadcasted_iota(jnp.int32, (PEER_SLOTS, 2 * PEER_SLOTS), 0)
    c = lax.broadcasted_iota(jnp.int32, (PEER_SLOTS, 2 * PEER_SLOTS), 1)
    spread = jnp.where(c // 2 == r, 1.0, 0.0).astype(BF16)
    ghi, glo = _split2(gates)
    tn = lambda a, b: lax.dot_general(a, b, (((0,), (0,)), ((), ())), preferred_element_type=F32)
    gate_ref[...] = tn(ghi, spread) + tn(glo, spread)


def _peer_route(h2, w_pq, sub_keys, tb=256):
    T, D = h2.shape
    wpq = w_pq.astype(BF16)
    sk = sub_keys.reshape(PEER_HEADS * 2, N_KEYS, PEER_HALF).astype(BF16)
    return pl.pallas_call(
        _peer_route_kernel,
        grid=(T // tb,),
        in_specs=[pl.BlockSpec((tb, D), lambda i: (i, 0)), _resident(), _resident()],
        out_specs=[pl.BlockSpec((tb, PEER_SLOTS), lambda i: (i, 0)),
                   pl.BlockSpec((tb, 2 * PEER_SLOTS), lambda i: (i, 0))],
        out_shape=[jax.ShapeDtypeStruct((T, PEER_SLOTS), jnp.int32),
                   jax.ShapeDtypeStruct((T, 2 * PEER_SLOTS), F32)],
        compiler_params=_cparams("parallel"),
        name="peer_route",
    )(h2, wpq, sk)


PEER_TB = 32
PEER_NBUF = 8
PEER_AHEAD = 5
PEER_ID_PAD = 8


def _peer_expert_kernel(ids_ref, tab_ref, gate_ref, h_ref, x1_ref, nf_ref, o_ref, buf, sem, *, final_norm):
    i = pl.program_id(0)
    nblk = pl.num_programs(0)
    tb, D = h_ref.shape
    K = PEER_SLOTS
    Dh = D // 2
    ntile = Dh // LANES
    per_tile = K // ntile

    def gather_part(row, slot, part):
        base = row * K
        for k in range(part * per_tile, (part + 1) * per_tile):
            pltpu.make_async_copy(tab_ref.at[ids_ref[base + k]], buf.at[slot, pl.ds(2 * k, 2), :],
                                  sem.at[slot]).start()

    def wait_rows(slot):
        pltpu.make_async_copy(buf.at[slot], buf.at[slot], sem.at[slot]).wait()

    @pl.when(i == 0)
    def _():
        for t in range(PEER_AHEAD):
            for part in range(ntile):
                gather_part(t, t, part)

    even_sub = (lax.broadcasted_iota(jnp.int32, (8, LANES), 0) % 2) == 0
    lane1 = lax.broadcasted_iota(jnp.int32, (1, 2 * K), 1)
    sub8 = lax.broadcasted_iota(jnp.int32, (8, 2 * K), 0)
    lane8 = lax.broadcasted_iota(jnp.int32, (8, 2 * K), 1)
    ones8 = jnp.ones((8, LANES), BF16)
    nt = lambda a, b: lax.dot_general(a, b, (((1,), (1,)), ((), ())), preferred_element_type=F32)

    def dots(row, slot, ahead_row, ahead_slot):
        hrow = h_ref[pl.ds(row, 1), :]
        a = None
        for p in range(ntile):
            gather_part(ahead_row, ahead_slot, p)
            lo = hrow[:, p * LANES:(p + 1) * LANES]
            hi = hrow[:, Dh + p * LANES:Dh + (p + 1) * LANES]
            hp = jnp.where(even_sub, lo, hi)
            w = buf[slot, :, p * LANES:(p + 1) * LANES].astype(F32).reshape(2 * K // 8, 8, LANES)
            term = w * hp[None]
            a = term if a is None else a + term
        return a.reshape(2 * K, LANES)

    def activate(row, a):
        ahi, alo = _split2(a)
        r = (nt(ones8, ahi) + nt(ones8, alo))[0:1]
        s = r + jnp.where(lane1 % 2 == 0, pltpu.roll(r, 2 * K - 1, axis=1), pltpu.roll(r, 1, axis=1))
        act = 0.5 * s * (1.0 + lax.erf(s * (2.0 ** -0.5))) * gate_ref[pl.ds(row, 1), :]
        return jnp.where(sub8 == lane8 % 2, act, 0.0).astype(BF16)

    def combine(row, slot, a8):
        o = [jnp.dot(a8, buf[slot, :, Dh + n * 256:Dh + (n + 1) * 256], preferred_element_type=F32)
             for n in range(Dh // 256)]
        y = x1_ref[pl.ds(row, 1), :] + jnp.concatenate([t[0:1] for t in o] + [t[1:2] for t in o], axis=1)
        if final_norm:
            ms = jnp.mean(y * y, axis=-1, keepdims=True)
            y = y * lax.rsqrt(ms + NORM_EPS) * nf_ref[...]
        o_ref[pl.ds(row, 1), :] = y

    def ring_turn(gi, c):
        j0 = gi * PEER_NBUF
        dotted = None
        activated = None
        for u in range(PEER_NBUF + 2):
            if u < PEER_NBUF:
                wait_rows(u)
            nxt = None
            if dotted is not None:
                nxt = (dotted[0], dotted[1], activate(dotted[0], dotted[2]))
            if activated is not None:
                combine(*activated)
            activated = nxt
            dotted = None
            if u < PEER_NBUF:
                dotted = (j0 + u, u, dots(j0 + u, u, j0 + u + PEER_AHEAD, (u + PEER_AHEAD) % PEER_NBUF))
        return c

    lax.fori_loop(0, tb // PEER_NBUF, ring_turn, 0)

    @pl.when(i == nblk - 1)
    def _():
        for t in range(PEER_AHEAD):
            wait_rows(t)


def _pack_tables(u_tab, v_tab):
    E, D = u_tab.shape
    halves = lambda t: t.astype(BF16).reshape(E, 2, 1, D // 2)
    return jnp.concatenate([halves(u_tab), halves(v_tab)], axis=2).reshape(E, 2, D)


def _peer_expert(ids, gate2, h2, x1, u_tab, v_tab, normf_w, final_norm):
    T, D = x1.shape
    tb = PEER_TB
    nblk = T // tb
    tab = _pack_tables(u_tab, v_tab)
    ids3 = ids.reshape(nblk, tb, PEER_SLOTS)
    nxt = jnp.concatenate([ids3[1:, :PEER_ID_PAD], jnp.zeros((1, PEER_ID_PAD, PEER_SLOTS), ids.dtype)], axis=0)
    ids_ext = jnp.concatenate([ids3, nxt], axis=1).reshape(-1)
    nid = (tb + PEER_ID_PAD) * PEER_SLOTS
    row = lambda n: pl.BlockSpec((tb, n), lambda i: (i, 0))
    return pl.pallas_call(
        functools.partial(_peer_expert_kernel, final_norm=final_norm),
        grid=(nblk,),
        in_specs=[pl.BlockSpec((nid,), lambda i: (i,), memory_space=pltpu.SMEM),
                  pl.BlockSpec(memory_space=pl.ANY),
                  row(2 * PEER_SLOTS), row(D), row(D), _resident()],
        out_specs=row(D),
        out_shape=jax.ShapeDtypeStruct((T, D), F32),
        scratch_shapes=[pltpu.VMEM((PEER_NBUF, 2 * PEER_SLOTS, D), BF16),
                        pltpu.SemaphoreType.DMA((PEER_NBUF,))],
        compiler_params=_cparams("arbitrary"),
        name="peer_expert",
    )(ids_ext, tab, gate2, h2, x1, normf_w.reshape(1, D))


def kernel(x, w_in, mu_prev, mu_next, w0, w2, a0, a2, g2, k_k, k_a, r_k, lnx_w, lnx_b, q_norm_w, k_norm_w, w_out, norm1_w, norm2_w, w_pq, sub_keys, u_tab, v_tab, normf_w):
    B, S, D = x.shape
    depth = w_in.shape[0]
    xf = x.reshape(B * S, D)
    for l in range(depth):
        z, q, kv = _inproj(xf, norm1_w[l], w_in[l])
        A, Bm, K, R, V, G, bonus, WL = _rwkv_prep(
            z.reshape(B, S, RWKV_COLS), mu_prev[l], mu_next[l], w0[l], w2[l], a0[l], a2[l], g2[l],
            k_k[l], k_a[l], r_k[l])
        y2 = _rwkv_scan(A, Bm, K, R, V, WL)
        qr, kr, vr = _attn_prep(q.reshape(B, S, ATTN_W), kv.reshape(B, S, 2 * KV_W), q_norm_w[l], k_norm_w[l])
        attn = _attention(qr, kr, vr)
        x1, h2 = _mix_out(y2.reshape(2, B * S, RWKV_W), bonus.reshape(B * S, RWKV_W), G.reshape(B * S, RWKV_W),
                          attn.reshape(B * S, ATTN_W), xf, w_out[l], lnx_w[l], lnx_b[l], norm2_w[l])
        ids, gate2 = _peer_route(h2, w_pq[l], sub_keys[l])
        xf = _peer_expert(ids, gate2, h2, x1, u_tab[l], v_tab[l], normf_w, final_norm=(l == depth - 1))
    return xf.reshape(B, S, D)
```
